```python
import math
import jax, jax.numpy as jnp
from jax import lax
import numpy as np

D_MODEL = 2048
BATCH = 4
SEQ = 2048
DEPTH = 4

D_MIX = D_MODEL
D_ATT = D_MIX // 2
D_CONV = D_MIX // 4
D_GMLP = D_MIX - D_ATT - D_CONV
ATT_V_DIM = 128
N_ATT_HEADS = D_ATT // ATT_V_DIM
ATT_QK_DIM = ATT_V_DIM // 2
Q_BLOCK = 128
CONV_WIDTH = 3
GMLP_CHUNK = 128
GMLP_GROUP = 128
N_GMLP_GROUPS = D_GMLP // GMLP_GROUP
D_IN = 3 * D_ATT + 3 * D_CONV + 2 * D_GMLP
N_EXPERTS = 64
N_EXPERT_GROUPS = 8
TOPK_GROUPS = 4
TOP_K = 8
D_EXPERT = D_MODEL // 4
D_SHARED = D_EXPERT
ROUTED_SCALE = 2.5
MOE_BLOCK = 128
LN_EPS = 1e-5
DEEPNORM_ALPHA = (2 * DEPTH) ** 0.25
DEEPNORM_BETA = (8 * DEPTH) ** -0.25

kernel_name = "hybrid_diffattn_shortconv_gmlp_moe_deepnorm"


def layer_norm(x, g, b):
    xf = x.astype(jnp.float32)
    mu = jnp.mean(xf, axis=-1, keepdims=True)
    var = jnp.mean(jnp.square(xf - mu), axis=-1, keepdims=True)
    y = (xf - mu) * lax.rsqrt(var + LN_EPS) * g.astype(jnp.float32) + b.astype(jnp.float32)
    return y.astype(x.dtype)


def rms_norm(x, g):
    xf = x.astype(jnp.float32)
    y = xf * lax.rsqrt(jnp.mean(jnp.square(xf), axis=-1, keepdims=True) + LN_EPS) * g.astype(jnp.float32)
    return y


def diff_attention(q, k, v, lam, lam_init, sub_g):
    B, S, _ = q.shape
    q = q.reshape(B, S, N_ATT_HEADS, 2, ATT_QK_DIM).transpose(0, 2, 3, 1, 4)
    k = k.reshape(B, S, N_ATT_HEADS, 2, ATT_QK_DIM).transpose(0, 2, 3, 1, 4)
    vh = v.reshape(B, S, N_ATT_HEADS, ATT_V_DIM).transpose(0, 2, 1, 3)
    scale = ATT_QK_DIM ** -0.5
    outs = []
    for i in range(S // Q_BLOCK):
        q0 = i * Q_BLOCK
        kl = q0 + Q_BLOCK
        qb = q[:, :, :, q0:kl].astype(jnp.float32)
        kb = k[:, :, :, :kl].astype(jnp.float32)
        s = jnp.einsum('bhjqd,bhjkd->bhjqk', qb, kb) * scale
        mask = (q0 + jnp.arange(Q_BLOCK))[:, None] >= jnp.arange(kl)[None, :]
        s = jnp.where(mask, s, -jnp.inf)
        p = jax.nn.softmax(s, axis=-1)
        a = p[:, :, 0] - lam * p[:, :, 1]
        outs.append(jnp.einsum('bhqk,bhkd->bhqd', a, vh[:, :, :kl].astype(jnp.float32)))
    o = jnp.concatenate(outs, axis=2)
    o = rms_norm(o, sub_g) * (1.0 - lam_init)
    return o.transpose(0, 2, 1, 3).reshape(B, S, D_ATT).astype(v.dtype)


def short_conv_mixer(b_gate, c_gate, h, conv_w):
    z = c_gate * h
    rhs = conv_w[:, None, :].astype(z.dtype)
    y = lax.conv_general_dilated(z, rhs, window_strides=(1,), padding=[(CONV_WIDTH - 1, 0)],
                                 dimension_numbers=('NWC', 'WIO', 'NWC'),
                                 feature_group_count=z.shape[-1])
    return b_gate * y


def chunked_spatial_gating(u, v, v_g, v_b, w_s, b_s):
    B, S, _ = v.shape
    v = layer_norm(v, v_g, v_b)
    vr = v.reshape(B, S // GMLP_CHUNK, GMLP_CHUNK, N_GMLP_GROUPS, GMLP_GROUP)
    causal = jnp.tril(jnp.ones((GMLP_CHUNK, GMLP_CHUNK), dtype=bool))
    w = jnp.where(causal[None], w_s, 0.0).astype(v.dtype)
    z = jnp.einsum('gts,bnsgc->bntgc', w, vr) + b_s.T.astype(v.dtype)[None, None, :, :, None]
    return u * z.reshape(B, S, D_GMLP)


def hybrid_mixer(x, w_in, conv_w, lam_q1, lam_k1, lam_q2, lam_k2, sub_g, v_g, v_b, w_s, b_s, w_out, lam_init):
    proj = x @ w_in
    cuts = np.cumsum([D_ATT, D_ATT, D_ATT, D_CONV, D_CONV, D_CONV, D_GMLP]).tolist()
    q, k, v, cb, cc, ch, gu, gv = jnp.split(proj, cuts, axis=-1)
    lam = (jnp.exp(jnp.sum(lam_q1.astype(jnp.float32) * lam_k1.astype(jnp.float32)))
           - jnp.exp(jnp.sum(lam_q2.astype(jnp.float32) * lam_k2.astype(jnp.float32))) + lam_init)
    y_att = diff_attention(q, k, v, lam, lam_init, sub_g)
    y_conv = short_conv_mixer(cb, cc, ch, conv_w)
    y_gmlp = chunked_spatial_gating(gu, gv, v_g, v_b, w_s, b_s)
    return jnp.concatenate([y_att, y_conv, y_gmlp], axis=-1) @ w_out


def moe(x, w_router, b_router, w_gate, w_up, w_down, ws_gate, ws_up, ws_down):
    B, S, D = x.shape
    N = B * S
    xt = x.reshape(N, D)
    scores = jax.nn.sigmoid((xt @ w_router).astype(jnp.float32))
    choice = scores + b_router.astype(jnp.float32)
    grp = choice.reshape(N, N_EXPERT_GROUPS, N_EXPERTS // N_EXPERT_GROUPS)
    grp_score = lax.top_k(grp, 2)[0].sum(-1)
    _, grp_idx = lax.top_k(grp_score, TOPK_GROUPS)
    grp_mask = jax.nn.one_hot(grp_idx, N_EXPERT_GROUPS, dtype=jnp.float32).sum(1) > 0
    expert_mask = jnp.repeat(grp_mask, N_EXPERTS // N_EXPERT_GROUPS, axis=1)
    _, top_idx = lax.top_k(jnp.where(expert_mask, choice, -jnp.inf), TOP_K)
    top_w = jnp.take_along_axis(scores, top_idx, axis=1)
    top_w = top_w / jnp.sum(top_w, axis=-1, keepdims=True) * ROUTED_SCALE

    NK = N * TOP_K
    flat_e = top_idx.reshape(NK).astype(jnp.int32)
    flat_t = jnp.repeat(jnp.arange(N, dtype=jnp.int32), TOP_K)
    flat_w = top_w.reshape(NK)
    order = jnp.argsort(flat_e)
    se = flat_e[order]
    counts = jnp.bincount(flat_e, length=N_EXPERTS).astype(jnp.int32)
    starts = jnp.cumsum(counts) - counts
    pcounts = (counts + MOE_BLOCK - 1) // MOE_BLOCK * MOE_BLOCK
    pends = jnp.cumsum(pcounts)
    pstarts = pends - pcounts
    n_blocks = -(-(NK + N_EXPERTS * (MOE_BLOCK - 1)) // MOE_BLOCK)
    P = n_blocks * MOE_BLOCK
    dest = pstarts[se] + jnp.arange(NK, dtype=jnp.int32) - starts[se]
    row_tok = jnp.full((P,), N, dtype=jnp.int32).at[dest].set(flat_t[order])
    row_w = jnp.zeros((P,), jnp.float32).at[dest].set(flat_w[order])
    block_e = jnp.minimum(jnp.searchsorted(pends, jnp.arange(n_blocks, dtype=jnp.int32) * MOE_BLOCK, side='right'),
                          N_EXPERTS - 1)
    x_pad = jnp.concatenate([xt, jnp.zeros((1, D), xt.dtype)], axis=0)
    xs = x_pad[row_tok].reshape(n_blocks, MOE_BLOCK, D)

    def expert_block(args):
        xb, e = args
        h = jax.nn.silu(xb @ w_gate[e]) * (xb @ w_up[e])
        return h @ w_down[e]

    ys = lax.map(expert_block, (xs, block_e)).reshape(P, D)
    routed = jax.ops.segment_sum(ys * row_w[:, None].astype(ys.dtype), row_tok, num_segments=N + 1)[:N]
    shared = (jax.nn.silu(xt @ ws_gate) * (xt @ ws_up)) @ ws_down
    return (routed + shared).reshape(B, S, D)


def setup_inputs(seed: int = 0) -> dict:
    key = jax.random.key(seed)
    ks = jax.random.split(key, 26)
    f32 = jnp.float32
    nrm = lambda k, shape, s: jax.random.normal(k, shape, f32) * s
    L, D = DEPTH, D_MODEL
    w_in = nrm(ks[1], (L, D, D_IN), D ** -0.5)
    w_in = w_in.at[..., 2 * D_ATT:3 * D_ATT].multiply(DEEPNORM_BETA)
    return {
        "x": jax.random.normal(ks[0], (BATCH, SEQ, D), f32),
        "w_in": w_in,
        "conv_w": nrm(ks[2], (L, CONV_WIDTH, D_CONV), CONV_WIDTH ** -0.5),
        "lam_q1": nrm(ks[3], (L, ATT_QK_DIM), 0.1),
        "lam_k1": nrm(ks[4], (L, ATT_QK_DIM), 0.1),
        "lam_q2": nrm(ks[5], (L, ATT_QK_DIM), 0.1),
        "lam_k2": nrm(ks[6], (L, ATT_QK_DIM), 0.1),
        "attn_sub_g": 1.0 + nrm(ks[7], (L, ATT_V_DIM), 0.02),
        "gmlp_v_g": 1.0 + nrm(ks[8], (L, D_GMLP), 0.02),
        "gmlp_v_b": nrm(ks[9], (L, D_GMLP), 0.02),
        "gmlp_w_s": nrm(ks[10], (L, N_GMLP_GROUPS, GMLP_CHUNK, GMLP_CHUNK), GMLP_CHUNK ** -0.5),
        "gmlp_b_s": 1.0 + nrm(ks[11], (L, N_GMLP_GROUPS, GMLP_CHUNK), 0.02),
        "w_out": nrm(ks[12], (L, D_MIX, D), D_MIX ** -0.5 * DEEPNORM_BETA),
        "ln1_g": 1.0 + nrm(ks[13], (L, D), 0.02),
        "ln1_b": nrm(ks[14], (L, D), 0.02),
        "w_router": nrm(ks[15], (L, D, N_EXPERTS), D ** -0.5),
        "b_router": nrm(ks[16], (L, N_EXPERTS), 0.01),
        "w_gate": nrm(ks[17], (L, N_EXPERTS, D, D_EXPERT), D ** -0.5),
        "w_up": nrm(ks[18], (L, N_EXPERTS, D, D_EXPERT), D ** -0.5),
        "w_down": nrm(ks[19], (L, N_EXPERTS, D_EXPERT, D), D_EXPERT ** -0.5 * DEEPNORM_BETA),
        "ws_gate": nrm(ks[20], (L, D, D_SHARED), D ** -0.5),
        "ws_up": nrm(ks[21], (L, D, D_SHARED), D ** -0.5),
        "ws_down": nrm(ks[22], (L, D_SHARED, D), D_SHARED ** -0.5 * DEEPNORM_BETA),
        "ln2_g": 1.0 + nrm(ks[23], (L, D), 0.02),
        "ln2_b": nrm(ks[24], (L, D), 0.02),
    }


def reference(x, w_in, conv_w, lam_q1, lam_k1, lam_q2, lam_k2, attn_sub_g, gmlp_v_g, gmlp_v_b,
              gmlp_w_s, gmlp_b_s, w_out, ln1_g, ln1_b, w_router, b_router, w_gate, w_up, w_down,
              ws_gate, ws_up, ws_down, ln2_g, ln2_b):
    for l in range(DEPTH):
        lam_init = 0.8 - 0.6 * math.exp(-0.3 * l)
        h = hybrid_mixer(x, w_in[l], conv_w[l], lam_q1[l], lam_k1[l], lam_q2[l], lam_k2[l], attn_sub_g[l],
                         gmlp_v_g[l], gmlp_v_b[l], gmlp_w_s[l], gmlp_b_s[l], w_out[l], lam_init)
        x = layer_norm(DEEPNORM_ALPHA * x + h, ln1_g[l], ln1_b[l])
        f = moe(x, w_router[l], b_router[l], w_gate[l], w_up[l], w_down[l], ws_gate[l], ws_up[l], ws_down[l])
        x = layer_norm(DEEPNORM_ALPHA * x + f, ln2_g[l], ln2_b[l])
    return x
```

```python
import functools
import math

import jax
import jax.numpy as jnp
from jax import lax
from jax.experimental import pallas as pl
from jax.experimental.pallas import tpu as pltpu

F32 = jnp.float32
BF16 = jnp.bfloat16
I32 = jnp.int32

DEPTH = 4
N_HEADS = 8
HEAD_DIM = 128
QK_HALF = 64
CONV_W = 3
CHUNK = 128
N_GROUPS_GMLP = 4
N_EXPERTS = 64
N_EXPERT_GROUPS = 8
GROUP_SIZE = N_EXPERTS // N_EXPERT_GROUPS
TOPK_GROUPS = 4
TOP_K = 8
ROUTED_SCALE = 2.5
LN_EPS = 1e-5
ALPHA = (2 * DEPTH) ** 0.25

SUBLANES = 8
LANES = 128
VMEM_LIMIT = 56 * 1024 * 1024

EXPERT_BM = 256
NEG_INF = float("-inf")


def _layer_norm(r, g, b):
    mu = jnp.mean(r, axis=-1, keepdims=True)
    c = r - mu
    var = jnp.mean(c * c, axis=-1, keepdims=True)
    return c * lax.rsqrt(var + LN_EPS) * g + b


def _params(sem, vmem=VMEM_LIMIT):
    return pltpu.CompilerParams(dimension_semantics=sem, vmem_limit_bytes=vmem)


def _proj_kernel(x_ref, w_ref, o_ref):
    o_ref[...] = jnp.dot(x_ref[...], w_ref[...], preferred_element_type=F32).astype(o_ref.dtype)


def _proj(x_bf, w_bf, out_dtype, bm, bn, name):
    m, k = x_bf.shape
    n = w_bf.shape[1]
    return pl.pallas_call(
        _proj_kernel,
        grid=(m // bm, n // bn),
        in_specs=[pl.BlockSpec((bm, k), lambda i, j: (i, 0)),
                  pl.BlockSpec((k, bn), lambda i, j: (0, j))],
        out_specs=pl.BlockSpec((bm, bn), lambda i, j: (i, j)),
        out_shape=jax.ShapeDtypeStruct((m, n), out_dtype),
        compiler_params=_params(("parallel", "parallel")),
        name=name,
    )(x_bf, w_bf)


def _attn_kernel(q_ref, k_ref, v_ref, lq1_ref, lk1_ref, lq2_ref, lk2_ref, g_ref, o_ref, *, lam_init, tq):
    i = pl.program_id(2)
    lam = (jnp.exp(jnp.sum(lq1_ref[...] * lk1_ref[...], axis=-1, keepdims=True))
           - jnp.exp(jnp.sum(lq2_ref[...] * lk2_ref[...], axis=-1, keepdims=True)) + lam_init)

    q = q_ref[...] * (QK_HALF ** -0.5)
    lane = lax.broadcasted_iota(I32, q.shape, 1)
    q0 = jnp.where(lane < QK_HALF, q, 0).astype(BF16)
    q1 = jnp.where(lane >= QK_HALF, q, 0).astype(BF16)
    nt = (((1,), (1,)), ((), ()))

    def half(qh, kb, vb, mask, m, l, acc):
        s = lax.dot_general(qh, kb, nt, preferred_element_type=F32)
        if mask is not None:
            s = jnp.where(mask, s, NEG_INF)
        m_new = jnp.maximum(m, jnp.max(s, axis=-1, keepdims=True))
        p = jnp.exp(s - m_new)
        a = jnp.exp(m - m_new)
        l = a * l + jnp.sum(p, axis=-1, keepdims=True)
        acc = a * acc + jnp.dot(p.astype(BF16), vb, preferred_element_type=F32)
        return m_new, l, acc

    def step(j, carry, mask):
        m0, l0, a0, m1, l1, a1 = carry
        off = pl.multiple_of(j * tq, tq)
        kb = k_ref[pl.ds(off, tq), :]
        vb = v_ref[pl.ds(off, tq), :]
        m0, l0, a0 = half(q0, kb, vb, mask, m0, l0, a0)
        m1, l1, a1 = half(q1, kb, vb, mask, m1, l1, a1)
        return m0, l0, a0, m1, l1, a1

    neg = jnp.full((tq, 1), NEG_INF, F32)
    zero1 = jnp.zeros((tq, 1), F32)
    zacc = jnp.zeros((tq, HEAD_DIM), F32)
    carry = (neg, zero1, zacc, neg, zero1, zacc)
    carry = lax.fori_loop(0, i, lambda j, c: step(j, c, None), carry)
    row = lax.broadcasted_iota(I32, (tq, tq), 0)
    col = lax.broadcasted_iota(I32, (tq, tq), 1)
    m0, l0, a0, m1, l1, a1 = step(i, carry, row >= col)

    o = a0 / l0 - lam * (a1 / l1)
    y = o * lax.rsqrt(jnp.mean(o * o, axis=-1, keepdims=True) + LN_EPS) * g_ref[...] * (1.0 - lam_init)
    o_ref[...] = y.astype(o_ref.dtype)


def _attention(qkv, lq1, lk1, lq2, lk2, sub_g, lam_init, batch, seq, tq=256):
    n = batch * seq
    nq = seq // tq
    vec = lambda d: pl.BlockSpec((1, d), lambda b, h, i: (0, 0))
    return pl.pallas_call(
        functools.partial(_attn_kernel, lam_init=lam_init, tq=tq),
        grid=(batch, N_HEADS, nq),
        in_specs=[pl.BlockSpec((tq, HEAD_DIM), lambda b, h, i: (b * nq + i, h)),
                  pl.BlockSpec((seq, HEAD_DIM), lambda b, h, i: (b, N_HEADS + h)),
                  pl.BlockSpec((seq, HEAD_DIM), lambda b, h, i: (b, 2 * N_HEADS + h)),
                  vec(QK_HALF), vec(QK_HALF), vec(QK_HALF), vec(QK_HALF), vec(HEAD_DIM)],
        out_specs=pl.BlockSpec((tq, HEAD_DIM), lambda b, h, i: (b * nq + i, h)),
        out_shape=jax.ShapeDtypeStruct((n, N_HEADS * HEAD_DIM), BF16),
        compiler_params=_params(("parallel", "parallel", "arbitrary")),
        name="attn",
    )(qkv, qkv, qkv, lq1, lk1, lq2, lk2, sub_g)


def _mix_kernel(cb_ref, cc_ref, ch_ref, cch_ref, chh_ref, gu_ref, gv_ref, cw_ref, vg_ref, vb_ref,
                ws_ref, bs_ref, yc_ref, yg_ref, *, ts):
    i = pl.program_id(1)
    z = cc_ref[...] * ch_ref[...]
    zh = cch_ref[...] * chh_ref[...]
    zh = jnp.where(i > 0, zh, 0.0)
    row = lax.broadcasted_iota(I32, z.shape, 0)
    zm1 = jnp.where(row == 0, zh[7:8, :], pltpu.roll(z, 1, 0))
    zm2 = jnp.where(row == 0, zh[6:7, :], jnp.where(row == 1, zh[7:8, :], pltpu.roll(z, 2, 0)))
    w = cw_ref[...]
    conv = w[0:1, :] * zm2 + w[1:2, :] * zm1 + w[2:3, :] * z
    yc_ref[...] = (cb_ref[...] * conv).astype(yc_ref.dtype)

    v = _layer_norm(gv_ref[...], vg_ref[...], vb_ref[...]).astype(BF16)
    r = lax.broadcasted_iota(I32, (CHUNK, CHUNK), 0)
    c = lax.broadcasted_iota(I32, (CHUNK, CHUNK), 1)
    for g in range(N_GROUPS_GMLP):
        wg = jnp.where(r >= c, ws_ref[g], 0.0).astype(BF16)
        bg = bs_ref[g]
        cols = slice(g * CHUNK, (g + 1) * CHUNK)
        for n in range(ts // CHUNK):
            rows = slice(n * CHUNK, (n + 1) * CHUNK)
            zz = jnp.dot(wg, v[rows, cols], preferred_element_type=F32) + bg
            yg_ref[rows, cols] = (gu_ref[rows, cols] * zz).astype(yg_ref.dtype)


def _mix(rest, conv_w, v_g, v_b, w_s, b_s, batch, seq, ts=512):
    n = batch * seq
    c = conv_w.shape[1]
    nt = seq // ts
    col = lambda j: pl.BlockSpec((ts, c), lambda b, i: (b * nt + i, j))
    halo = lambda j: pl.BlockSpec(
        (SUBLANES, c), lambda b, i: (jnp.maximum((b * seq + i * ts) // SUBLANES - 1, 0), j))
    full = lambda a: pl.BlockSpec(a.shape, lambda b, i: (0,) * a.ndim)
    out = pl.BlockSpec((ts, c), lambda b, i: (b * nt + i, 0))
    return pl.pallas_call(
        functools.partial(_mix_kernel, ts=ts),
        grid=(batch, nt),
        in_specs=[col(0), col(1), col(2), halo(1), halo(2), col(3), col(4),
                  full(conv_w), full(v_g), full(v_b), full(w_s), full(b_s)],
        out_specs=[out, out],
        out_shape=[jax.ShapeDtypeStruct((n, c), BF16), jax.ShapeDtypeStruct((n, c), BF16)],
        compiler_params=_params(("parallel", "arbitrary")),
        name="mix",
    )(rest, rest, rest, rest, rest, rest, rest, conv_w, v_g, v_b, w_s, b_s)


def _outproj_kernel(ya_ref, yc_ref, yg_ref, x_ref, w_ref, g_ref, b_ref, o_ref, ob_ref):
    da = ya_ref.shape[1]
    dc = yc_ref.shape[1]
    h = jnp.dot(ya_ref[...], w_ref[0:da, :], preferred_element_type=F32)
    h += jnp.dot(yc_ref[...], w_ref[da:da + dc, :], preferred_element_type=F32)
    h += jnp.dot(yg_ref[...], w_ref[da + dc:, :], preferred_element_type=F32)
    y = _layer_norm(ALPHA * x_ref[...] + h, g_ref[...], b_ref[...])
    o_ref[...] = y
    ob_ref[...] = y.astype(BF16)


def _outproj(ya, yc, yg, x, w_bf, g, b, tm=256):
    n, d = x.shape
    rows = lambda a: pl.BlockSpec((tm, a.shape[1]), lambda i: (i, 0))
    full = lambda a: pl.BlockSpec(a.shape, lambda i: (0,) * a.ndim)
    return pl.pallas_call(
        _outproj_kernel,
        grid=(n // tm,),
        in_specs=[rows(ya), rows(yc), rows(yg), rows(x), full(w_bf), full(g), full(b)],
        out_specs=[rows(x), rows(x)],
        out_shape=[jax.ShapeDtypeStruct((n, d), F32), jax.ShapeDtypeStruct((n, d), BF16)],
        compiler_params=_params(("parallel",)),
        name="outproj",
    )(ya, yc, yg, x, w_bf, g, b)


def _router_kernel(x_ref, wr_ref, br_ref, idx_ref, w_ref, rank_ref, cnt_ref, carry_ref, *, tt):
    i = pl.program_id(0)

    @pl.when(i == 0)
    def _():
        carry_ref[...] = jnp.zeros_like(carry_ref)

    logits = lax.dot_general(wr_ref[...], x_ref[...], (((1,), (1,)), ((), ())),
                             precision=lax.Precision.HIGHEST, preferred_element_type=F32)
    scores = jax.nn.sigmoid(logits)
    choice = scores + br_ref[...]
    eidx = lax.broadcasted_iota(I32, (N_EXPERTS, tt), 0)
    gidx = eidx // GROUP_SIZE

    c3 = choice.reshape(N_EXPERT_GROUPS, GROUP_SIZE, tt)
    e3 = lax.broadcasted_iota(I32, c3.shape, 1)
    m1 = jnp.max(c3, axis=1, keepdims=True)
    first = jnp.min(jnp.where(c3 == m1, e3, GROUP_SIZE), axis=1, keepdims=True)
    m2 = jnp.max(jnp.where(e3 == first, NEG_INF, c3), axis=1, keepdims=True)
    gscore = jnp.broadcast_to(m1 + m2, c3.shape).reshape(N_EXPERTS, tt)

    gsel = jnp.zeros((N_EXPERTS, tt), jnp.bool_)
    cur = gscore
    for _ in range(TOPK_GROUPS):
        m = jnp.max(cur, axis=0, keepdims=True)
        f = jnp.min(jnp.where(cur == m, gidx, N_EXPERT_GROUPS), axis=0, keepdims=True)
        hit = gidx == f
        gsel = jnp.logical_or(gsel, hit)
        cur = jnp.where(hit, NEG_INF, cur)

    cur = jnp.where(gsel, choice, NEG_INF)
    hits, sel_idx, sel_score = [], [], []
    for _ in range(TOP_K):
        m = jnp.max(cur, axis=0, keepdims=True)
        f = jnp.min(jnp.where(cur == m, eidx, N_EXPERTS), axis=0, keepdims=True)
        hit = eidx == f
        hits.append(hit)
        sel_idx.append(f)
        sel_score.append(jnp.sum(jnp.where(hit, scores, 0.0), axis=0, keepdims=True))
        cur = jnp.where(hit, NEG_INF, cur)

    total = sel_score[0]
    for s in sel_score[1:]:
        total = total + s

    multi = hits[0]
    for h in hits[1:]:
        multi = jnp.logical_or(multi, h)
    multi_f = jnp.where(multi, 1.0, 0.0)
    tr = lax.broadcasted_iota(I32, (tt, tt), 0)
    tc = lax.broadcasted_iota(I32, (tt, tt), 1)
    upper = jnp.where(tr < tc, 1.0, 0.0).astype(BF16)
    prefix = jnp.dot(multi_f.astype(BF16), upper, preferred_element_type=F32)
    base = carry_ref[...] + prefix
    for k in range(TOP_K):
        idx_ref[k:k + 1, :] = sel_idx[k]
        w_ref[k:k + 1, :] = sel_score[k] / total * ROUTED_SCALE
        rank_ref[k:k + 1, :] = jnp.sum(jnp.where(hits[k], base, 0.0), axis=0, keepdims=True).astype(I32)
    new_carry = carry_ref[...] + jnp.sum(multi_f, axis=1, keepdims=True)
    carry_ref[...] = new_carry
    cnt_ref[...] = new_carry.astype(I32)


def _router(x, wr_t, br, tt=512):
    n, d = x.shape
    tok = pl.BlockSpec((TOP_K, tt), lambda i: (0, i))
    return pl.pallas_call(
        functools.partial(_router_kernel, tt=tt),
        grid=(n // tt,),
        in_specs=[pl.BlockSpec((tt, d), lambda i: (i, 0)),
                  pl.BlockSpec(wr_t.shape, lambda i: (0, 0)),
                  pl.BlockSpec(br.shape, lambda i: (0, 0))],
        out_specs=[tok, tok, tok, pl.BlockSpec((N_EXPERTS, 1), lambda i: (0, 0))],
        out_shape=[jax.ShapeDtypeStruct((TOP_K, n), I32), jax.ShapeDtypeStruct((TOP_K, n), F32),
                   jax.ShapeDtypeStruct((TOP_K, n), I32), jax.ShapeDtypeStruct((N_EXPERTS, 1), I32)],
        scratch_shapes=[pltpu.VMEM((N_EXPERTS, 1), F32)],
        compiler_params=_params(("arbitrary",)),
        name="router",
    )(x, wr_t, br)


def _row_copy(src_ref, src_row, dst_ref, dst_row, sem):
    return pltpu.make_async_copy(src_ref.at[pl.ds(src_row, 1)], dst_ref.at[pl.ds(dst_row, 1)], sem)


def _dispatch_kernel(pstart_ref, pcnt_ref, nu_ref, idx_ref, rank_ref, x_ref, xs_ref, zero_ref, sem, *, tt, bm):
    i = pl.program_id(0)

    @pl.when(i == 0)
    def _():
        zero_ref[...] = jnp.zeros_like(zero_ref)

        def block_copy(row):
            return pltpu.make_async_copy(zero_ref, xs_ref.at[pl.ds(pl.multiple_of(row, bm), bm)], sem)

        def pad_copy(e):
            return block_copy(pstart_ref[e] + pcnt_ref[e] - bm)

        def start_pad(e, c):
            @pl.when(pcnt_ref[e] > 0)
            def _():
                pad_copy(e).start()
            return c

        def wait_pad(e, c):
            @pl.when(pcnt_ref[e] > 0)
            def _():
                pad_copy(e).wait()
            return c

        def start_tail(b, c):
            block_copy(b * bm).start()
            return c

        def wait_tail(b, c):
            block_copy(b * bm).wait()
            return c

        n_blocks = xs_ref.shape[0] // bm
        lax.fori_loop(0, N_EXPERTS, start_pad, 0)
        lax.fori_loop(nu_ref[0], n_blocks, start_tail, 0)
        lax.fori_loop(0, N_EXPERTS, wait_pad, 0)
        lax.fori_loop(nu_ref[0], n_blocks, wait_tail, 0)

    def start_rows(t, c):
        for k in range(TOP_K):
            dst = pstart_ref[idx_ref[k, t]] + rank_ref[k, t]
            _row_copy(x_ref, t, xs_ref, dst, sem).start()
        return c

    def wait_rows(t, c):
        for k in range(TOP_K):
            _row_copy(x_ref, 0, xs_ref, 0, sem).wait()
        return c

    lax.fori_loop(0, tt, start_rows, 0)
    lax.fori_loop(0, tt, wait_rows, 0)


def _dispatch(pstart, pcnt, n_used, idx, rank, x, p_rows, tt=256, bm=EXPERT_BM):
    n, d = x.shape
    smem_tok = pl.BlockSpec((TOP_K, tt), lambda i, ps, pc, nu: (0, i), memory_space=pltpu.SMEM)
    grid_spec = pltpu.PrefetchScalarGridSpec(
        num_scalar_prefetch=3,
        grid=(n // tt,),
        in_specs=[smem_tok, smem_tok, pl.BlockSpec((tt, d), lambda i, ps, pc, nu: (i, 0))],
        out_specs=pl.BlockSpec(memory_space=pl.ANY),
        scratch_shapes=[pltpu.VMEM((bm, d), x.dtype), pltpu.SemaphoreType.DMA(())],
    )
    return pl.pallas_call(
        functools.partial(_dispatch_kernel, tt=tt, bm=bm),
        grid_spec=grid_spec,
        out_shape=jax.ShapeDtypeStruct((p_rows, d), x.dtype),
        compiler_params=_params(("arbitrary",)),
        name="dispatch",
    )(pstart, pcnt, n_used, idx, rank, x)


def _expert_kernel(be_ref, nu_ref, x_ref, wg_ref, wu_ref, wd_ref, y_ref, wg_bf, wu_bf, wd_bf):
    b = pl.program_id(0)
    e = be_ref[b]
    prev = be_ref[jnp.maximum(b - 1, 0)]

    @pl.when(jnp.logical_or(b == 0, e != prev))
    def _():
        wg_bf[...] = wg_ref[0].astype(BF16)
        wu_bf[...] = wu_ref[0].astype(BF16)
        wd_bf[...] = wd_ref[0].astype(BF16)

    @pl.when(b < nu_ref[0])
    def _():
        x = x_ref[...].astype(BF16)
        g = jnp.dot(x, wg_bf[...], preferred_element_type=F32)
        u = jnp.dot(x, wu_bf[...], preferred_element_type=F32)
        h = (g * jax.nn.sigmoid(g)) * u
        y_ref[...] = jnp.dot(h.astype(BF16), wd_bf[...], preferred_element_type=F32)

    @pl.when(b >= nu_ref[0])
    def _():
        y_ref[...] = jnp.zeros_like(y_ref)


def _experts(block_e, n_used, xs, w_gate, w_up, w_down, bm=EXPERT_BM):
    p_rows, d = xs.shape
    de = w_gate.shape[2]
    wspec = lambda a: pl.BlockSpec((1,) + a.shape[1:], lambda b, be, nu: (be[b], 0, 0))
    grid_spec = pltpu.PrefetchScalarGridSpec(
        num_scalar_prefetch=2,
        grid=(p_rows // bm,),
        in_specs=[pl.BlockSpec((bm, d), lambda b, be, nu: (jnp.minimum(b, nu[0] - 1), 0)),
                  wspec(w_gate), wspec(w_up), wspec(w_down)],
        out_specs=pl.BlockSpec((bm, d), lambda b, be, nu: (b, 0)),
        scratch_shapes=[pltpu.VMEM((d, de), BF16), pltpu.VMEM((d, de), BF16), pltpu.VMEM((de, d), BF16)],
    )
    return pl.pallas_call(
        _expert_kernel,
        grid_spec=grid_spec,
        out_shape=jax.ShapeDtypeStruct((p_rows, d), F32),
        compiler_params=_params(("arbitrary",)),
        name="expert",
    )(block_e, n_used, xs, w_gate, w_up, w_down)


def _combine_kernel(pstart_ref, idx_ref, rank_ref, x_ref, xb_ref, w_ref, ys_ref, sg_ref, su_ref, sd_ref,
                    g_ref, b_ref, o_ref, ob_ref, buf, sem, *, tt):
    def start_rows(t, c):
        for k in range(TOP_K):
            src = pstart_ref[idx_ref[k, t]] + rank_ref[k, t]
            _row_copy(ys_ref, src, buf.at[k], t, sem).start()
        return c

    def wait_rows(t, c):
        for k in range(TOP_K):
            _row_copy(ys_ref, 0, buf.at[k], 0, sem).wait()
        return c

    lax.fori_loop(0, tt, start_rows, 0)

    xb = xb_ref[...]
    g = jnp.dot(xb, sg_ref[...], preferred_element_type=F32)
    u = jnp.dot(xb, su_ref[...], preferred_element_type=F32)
    h = (g * jax.nn.sigmoid(g)) * u
    shared = jnp.dot(h.astype(BF16), sd_ref[...], preferred_element_type=F32)

    lax.fori_loop(0, tt, wait_rows, 0)

    w = w_ref[...]
    routed = w[:, 0:1] * buf[0]
    for k in range(1, TOP_K):
        routed += w[:, k:k + 1] * buf[k]
    y = _layer_norm(ALPHA * x_ref[...] + (routed + shared), g_ref[...], b_ref[...])
    o_ref[...] = y
    ob_ref[...] = y.astype(BF16)


def _combine(pstart, idx, rank, x, x_bf, w_tok, ys, sg, su, sd, g, b, tt=128):
    n, d = x.shape
    smem_tok = pl.BlockSpec((TOP_K, tt), lambda i, ps: (0, i), memory_space=pltpu.SMEM)
    rows = lambda a: pl.BlockSpec((tt, a.shape[1]), lambda i, ps: (i, 0))
    full = lambda a: pl.BlockSpec(a.shape, lambda i, ps: (0,) * a.ndim)
    grid_spec = pltpu.PrefetchScalarGridSpec(
        num_scalar_prefetch=1,
        grid=(n // tt,),
        in_specs=[smem_tok, smem_tok, rows(x), rows(x_bf), rows(w_tok), pl.BlockSpec(memory_space=pl.ANY),
                  full(sg), full(su), full(sd), full(g), full(b)],
        out_specs=[rows(x), rows(x)],
        scratch_shapes=[pltpu.VMEM((TOP_K, tt, d), F32), pltpu.SemaphoreType.DMA(())],
    )
    return pl.pallas_call(
        functools.partial(_combine_kernel, tt=tt),
        grid_spec=grid_spec,
        out_shape=[jax.ShapeDtypeStruct((n, d), F32), jax.ShapeDtypeStruct((n, d), BF16)],
        compiler_params=_params(("arbitrary",)),
        name="combine",
    )(pstart, idx, rank, x, x_bf, w_tok, ys, sg, su, sd, g, b)


def _block_plan(counts, n_pairs, bm=EXPERT_BM):
    n_blocks = -(-(n_pairs + N_EXPERTS * (bm - 1)) // bm)
    pcnt = (counts + bm - 1) // bm * bm
    pend = jnp.cumsum(pcnt)
    pstart = pend - pcnt
    n_used = pend[-1] // bm
    blk = jnp.arange(n_blocks, dtype=I32)
    block_e = jnp.searchsorted(pend, jnp.minimum(blk, n_used - 1) * bm, side="right").astype(I32)
    block_e = jnp.minimum(block_e, N_EXPERTS - 1)
    return pstart.astype(I32), pcnt.astype(I32), block_e, n_used.astype(I32).reshape(1), n_blocks * bm


def kernel(x, w_in, conv_w, lam_q1, lam_k1, lam_q2, lam_k2, attn_sub_g, gmlp_v_g, gmlp_v_b, gmlp_w_s, gmlp_b_s,
           w_out, ln1_g, ln1_b, w_router, b_router, w_gate, w_up, w_down, ws_gate, ws_up, ws_down, ln2_g, ln2_b):
    batch, seq, d = x.shape
    n = batch * seq
    d_att = N_HEADS * HEAD_DIM
    xf = x.reshape(n, d)
    xb = xf.astype(BF16)
    for l in range(DEPTH):
        lam_init = 0.8 - 0.6 * math.exp(-0.3 * l)
        w_in_bf = w_in[l].astype(BF16)
        qkv = _proj(xb, w_in_bf[:, :3 * d_att], BF16, 1024, 512, "proj_qkv")
        rest = _proj(xb, w_in_bf[:, 3 * d_att:], F32, 1024, 512, "proj_rest")
        ya = _attention(qkv, lam_q1[l][None], lam_k1[l][None], lam_q2[l][None], lam_k2[l][None],
                        attn_sub_g[l][None], lam_init, batch, seq)
        yc, yg = _mix(rest, conv_w[l], gmlp_v_g[l][None], gmlp_v_b[l][None], gmlp_w_s[l],
                      gmlp_b_s[l][:, :, None], batch, seq)
        x1, x1b = _outproj(ya, yc, yg, xf, w_out[l].astype(BF16), ln1_g[l][None], ln1_b[l][None])

        idx, w_sel, rank, counts = _router(x1, w_router[l].T, b_router[l][:, None])
        pstart, pcnt, block_e, n_used, p_rows = _block_plan(counts[:, 0], n * TOP_K)
        xs = _dispatch(pstart, pcnt, n_used, idx, rank, x1, p_rows)
        ys = _experts(block_e, n_used, xs, w_gate[l], w_up[l], w_down[l])
        xf, xb = _combine(pstart, idx, rank, x1, x1b, w_sel.T, ys,
                          ws_gate[l].astype(BF16), ws_up[l].astype(BF16), ws_down[l].astype(BF16),
                          ln2_g[l][None], ln2_b[l][None])
    return xf.reshape(batch, seq, d)
```

```python
import functools
import math

import jax
import jax.numpy as jnp
from jax import lax
from jax.experimental import pallas as pl
from jax.experimental.pallas import tpu as pltpu

F32 = jnp.float32
BF16 = jnp.bfloat16
I32 = jnp.int32

DEPTH = 4
N_HEADS = 8
HEAD_DIM = 128
QK_HALF = 64
CONV_W = 3
CHUNK = 128
N_GROUPS_GMLP = 4
N_EXPERTS = 64
N_EXPERT_GROUPS = 8
GROUP_SIZE = N_EXPERTS // N_EXPERT_GROUPS
TOPK_GROUPS = 4
TOP_K = 8
ROUTED_SCALE = 2.5
LN_EPS = 1e-5
ALPHA = (2 * DEPTH) ** 0.25

SUBLANES = 8
LANES = 128
VMEM_LIMIT = 56 * 1024 * 1024

EXPERT_BM = 256
NEG_INF = float("-inf")


def _layer_norm(r, g, b):
    mu = jnp.mean(r, axis=-1, keepdims=True)
    c = r - mu
    var = jnp.mean(c * c, axis=-1, keepdims=True)
    return c * lax.rsqrt(var + LN_EPS) * g + b


def _params(sem, vmem=VMEM_LIMIT):
    return pltpu.CompilerParams(dimension_semantics=sem, vmem_limit_bytes=vmem)


def _proj_kernel(x_ref, w_ref, o_ref):
    o_ref[...] = jnp.dot(x_ref[...], w_ref[...], preferred_element_type=F32).astype(o_ref.dtype)


def _proj(x_bf, w_bf, out_dtype, bm, bn, name):
    m, k = x_bf.shape
    n = w_bf.shape[1]
    return pl.pallas_call(
        _proj_kernel,
        grid=(m // bm, n // bn),
        in_specs=[pl.BlockSpec((bm, k), lambda i, j: (i, 0)),
                  pl.BlockSpec((k, bn), lambda i, j: (0, j))],
        out_specs=pl.BlockSpec((bm, bn), lambda i, j: (i, j)),
        out_shape=jax.ShapeDtypeStruct((m, n), out_dtype),
        compiler_params=_params(("parallel", "parallel")),
        name=name,
    )(x_bf, w_bf)


def _attn_kernel(q_ref, k_ref, v_ref, lq1_ref, lk1_ref, lq2_ref, lk2_ref, g_ref, o_ref, *, lam_init, tq):
    i = pl.program_id(2)
    lam = (jnp.exp(jnp.sum(lq1_ref[...] * lk1_ref[...], axis=-1, keepdims=True))
           - jnp.exp(jnp.sum(lq2_ref[...] * lk2_ref[...], axis=-1, keepdims=True)) + lam_init)

    q = q_ref[...] * (QK_HALF ** -0.5)
    lane = lax.broadcasted_iota(I32, q.shape, 1)
    q0 = jnp.where(lane < QK_HALF, q, 0).astype(BF16)
    q1 = jnp.where(lane >= QK_HALF, q, 0).astype(BF16)
    nt = (((1,), (1,)), ((), ()))

    ones = jnp.ones((tq, HEAD_DIM), BF16)

    def half(qh, kb, v_ext, mask, m, acc):
        s = lax.dot_general(qh, kb, nt, preferred_element_type=F32)
        if mask is not None:
            s = jnp.where(mask, s, NEG_INF)
        m_new = jnp.maximum(m, jnp.max(s, axis=-1, keepdims=True))
        p = jnp.exp(s - m_new)
        acc = jnp.exp(m - m_new) * acc + jnp.dot(p.astype(BF16), v_ext, preferred_element_type=F32)
        return m_new, acc

    def step(j, carry, mask):
        m0, a0, m1, a1 = carry
        off = pl.multiple_of(j * tq, tq)
        kb = k_ref[pl.ds(off, tq), :]
        v_ext = jnp.concatenate([v_ref[pl.ds(off, tq), :], ones], axis=1)
        m0, a0 = half(q0, kb, v_ext, mask, m0, a0)
        m1, a1 = half(q1, kb, v_ext, mask, m1, a1)
        return m0, a0, m1, a1

    neg = jnp.full((tq, 1), NEG_INF, F32)
    zacc = jnp.zeros((tq, 2 * HEAD_DIM), F32)
    carry = (neg, zacc, neg, zacc)
    carry = lax.fori_loop(0, i, lambda j, c: step(j, c, None), carry)
    row = lax.broadcasted_iota(I32, (tq, tq), 0)
    col = lax.broadcasted_iota(I32, (tq, tq), 1)
    m0, a0, m1, a1 = step(i, carry, row >= col)

    o = a0[:, :HEAD_DIM] / a0[:, HEAD_DIM:] - lam * (a1[:, :HEAD_DIM] / a1[:, HEAD_DIM:])
    y = o * lax.rsqrt(jnp.mean(o * o, axis=-1, keepdims=True) + LN_EPS) * g_ref[...] * (1.0 - lam_init)
    o_ref[...] = y.astype(o_ref.dtype)


def _attention(qkv, lq1, lk1, lq2, lk2, sub_g, lam_init, batch, seq, tq=512):
    n = batch * seq
    nq = seq // tq
    vec = lambda d: pl.BlockSpec((1, d), lambda b, h, i: (0, 0))
    return pl.pallas_call(
        functools.partial(_attn_kernel, lam_init=lam_init, tq=tq),
        grid=(batch, N_HEADS, nq),
        in_specs=[pl.BlockSpec((tq, HEAD_DIM), lambda b, h, i: (b * nq + i, h)),
                  pl.BlockSpec((seq, HEAD_DIM), lambda b, h, i: (b, N_HEADS + h)),
                  pl.BlockSpec((seq, HEAD_DIM), lambda b, h, i: (b, 2 * N_HEADS + h)),
                  vec(QK_HALF), vec(QK_HALF), vec(QK_HALF), vec(QK_HALF), vec(HEAD_DIM)],
        out_specs=pl.BlockSpec((tq, HEAD_DIM), lambda b, h, i: (b * nq + i, h)),
        out_shape=jax.ShapeDtypeStruct((n, N_HEADS * HEAD_DIM), BF16),
        compiler_params=_params(("parallel", "parallel", "arbitrary")),
        name="attn",
    )(qkv, qkv, qkv, lq1, lk1, lq2, lk2, sub_g)


def _mix_kernel(cb_ref, cc_ref, ch_ref, cch_ref, chh_ref, gu_ref, gv_ref, cw_ref, vg_ref, vb_ref,
                ws_ref, bs_ref, yc_ref, yg_ref, *, ts):
    i = pl.program_id(1)
    z = cc_ref[...] * ch_ref[...]
    zh = cch_ref[...] * chh_ref[...]
    zh = jnp.where(i > 0, zh, 0.0)
    row = lax.broadcasted_iota(I32, z.shape, 0)
    zm1 = jnp.where(row == 0, zh[7:8, :], pltpu.roll(z, 1, 0))
    zm2 = jnp.where(row == 0, zh[6:7, :], jnp.where(row == 1, zh[7:8, :], pltpu.roll(z, 2, 0)))
    w = cw_ref[...]
    conv = w[0:1, :] * zm2 + w[1:2, :] * zm1 + w[2:3, :] * z
    yc_ref[...] = (cb_ref[...] * conv).astype(yc_ref.dtype)

    v = _layer_norm(gv_ref[...], vg_ref[...], vb_ref[...]).astype(BF16)
    r = lax.broadcasted_iota(I32, (CHUNK, CHUNK), 0)
    c = lax.broadcasted_iota(I32, (CHUNK, CHUNK), 1)
    for g in range(N_GROUPS_GMLP):
        wg = jnp.where(r >= c, ws_ref[g], 0.0).astype(BF16)
        bg = bs_ref[g]
        cols = slice(g * CHUNK, (g + 1) * CHUNK)
        for n in range(ts // CHUNK):
            rows = slice(n * CHUNK, (n + 1) * CHUNK)
            zz = jnp.dot(wg, v[rows, cols], preferred_element_type=F32) + bg
            yg_ref[rows, cols] = (gu_ref[rows, cols] * zz).astype(yg_ref.dtype)


def _mix(rest, conv_w, v_g, v_b, w_s, b_s, batch, seq, ts=512):
    n = batch * seq
    c = conv_w.shape[1]
    nt = seq // ts
    col = lambda j: pl.BlockSpec((ts, c), lambda b, i: (b * nt + i, j))
    halo = lambda j: pl.BlockSpec(
        (SUBLANES, c), lambda b, i: (jnp.maximum((b * seq + i * ts) // SUBLANES - 1, 0), j))
    full = lambda a: pl.BlockSpec(a.shape, lambda b, i: (0,) * a.ndim)
    out = pl.BlockSpec((ts, c), lambda b, i: (b * nt + i, 0))
    return pl.pallas_call(
        functools.partial(_mix_kernel, ts=ts),
        grid=(batch, nt),
        in_specs=[col(0), col(1), col(2), halo(1), halo(2), col(3), col(4),
                  full(conv_w), full(v_g), full(v_b), full(w_s), full(b_s)],
        out_specs=[out, out],
        out_shape=[jax.ShapeDtypeStruct((n, c), BF16), jax.ShapeDtypeStruct((n, c), BF16)],
        compiler_params=_params(("parallel", "arbitrary")),
        name="mix",
    )(rest, rest, rest, rest, rest, rest, rest, conv_w, v_g, v_b, w_s, b_s)


def _outproj_kernel(ya_ref, yc_ref, yg_ref, x_ref, w_ref, g_ref, b_ref, o_ref, ob_ref):
    da = ya_ref.shape[1]
    dc = yc_ref.shape[1]
    h = jnp.dot(ya_ref[...], w_ref[0:da, :], preferred_element_type=F32)
    h += jnp.dot(yc_ref[...], w_ref[da:da + dc, :], preferred_element_type=F32)
    h += jnp.dot(yg_ref[...], w_ref[da + dc:, :], preferred_element_type=F32)
    y = _layer_norm(ALPHA * x_ref[...] + h, g_ref[...], b_ref[...])
    o_ref[...] = y
    ob_ref[...] = y.astype(BF16)


def _outproj(ya, yc, yg, x, w_bf, g, b, tm=256):
    n, d = x.shape
    rows = lambda a: pl.BlockSpec((tm, a.shape[1]), lambda i: (i, 0))
    full = lambda a: pl.BlockSpec(a.shape, lambda i: (0,) * a.ndim)
    return pl.pallas_call(
        _outproj_kernel,
        grid=(n // tm,),
        in_specs=[rows(ya), rows(yc), rows(yg), rows(x), full(w_bf), full(g), full(b)],
        out_specs=[rows(x), rows(x)],
        out_shape=[jax.ShapeDtypeStruct((n, d), F32), jax.ShapeDtypeStruct((n, d), BF16)],
        compiler_params=_params(("parallel",)),
        name="outproj",
    )(ya, yc, yg, x, w_bf, g, b)


def _router_kernel(x_ref, wr_ref, br_ref, idx_ref, w_ref, rank_ref, cnt_ref, carry_ref, *, tt):
    i = pl.program_id(0)

    @pl.when(i == 0)
    def _():
        carry_ref[...] = jnp.zeros_like(carry_ref)

    logits = lax.dot_general(wr_ref[...], x_ref[...], (((1,), (1,)), ((), ())),
                             precision=lax.Precision.HIGHEST, preferred_element_type=F32)
    scores = jax.nn.sigmoid(logits)
    choice = scores + br_ref[...]
    eidx = lax.broadcasted_iota(I32, (N_EXPERTS, tt), 0)
    gidx = eidx // GROUP_SIZE

    c3 = choice.reshape(N_EXPERT_GROUPS, GROUP_SIZE, tt)
    e3 = lax.broadcasted_iota(I32, c3.shape, 1)
    m1 = jnp.max(c3, axis=1, keepdims=True)
    first = jnp.min(jnp.where(c3 == m1, e3, GROUP_SIZE), axis=1, keepdims=True)
    m2 = jnp.max(jnp.where(e3 == first, NEG_INF, c3), axis=1, keepdims=True)
    gscore = jnp.broadcast_to(m1 + m2, c3.shape).reshape(N_EXPERTS, tt)

    gsel = jnp.zeros((N_EXPERTS, tt), jnp.bool_)
    cur = gscore
    for _ in range(TOPK_GROUPS):
        m = jnp.max(cur, axis=0, keepdims=True)
        f = jnp.min(jnp.where(cur == m, gidx, N_EXPERT_GROUPS), axis=0, keepdims=True)
        hit = gidx == f
        gsel = jnp.logical_or(gsel, hit)
        cur = jnp.where(hit, NEG_INF, cur)

    cur = jnp.where(gsel, choice, NEG_INF)
    hits, sel_idx, sel_score = [], [], []
    for _ in range(TOP_K):
        m = jnp.max(cur, axis=0, keepdims=True)
        f = jnp.min(jnp.where(cur == m, eidx, N_EXPERTS), axis=0, keepdims=True)
        hit = eidx == f
        hits.append(hit)
        sel_idx.append(f)
        sel_score.append(jnp.sum(jnp.where(hit, scores, 0.0), axis=0, keepdims=True))
        cur = jnp.where(hit, NEG_INF, cur)

    total = sel_score[0]
    for s in sel_score[1:]:
        total = total + s

    multi = hits[0]
    for h in hits[1:]:
        multi = jnp.logical_or(multi, h)
    multi_f = jnp.where(multi, 1.0, 0.0)
    tr = lax.broadcasted_iota(I32, (tt, tt), 0)
    tc = lax.broadcasted_iota(I32, (tt, tt), 1)
    upper = jnp.where(tr < tc, 1.0, 0.0).astype(BF16)
    prefix = jnp.dot(multi_f.astype(BF16), upper, preferred_element_type=F32)
    base = carry_ref[...] + prefix
    for k in range(TOP_K):
        idx_ref[k:k + 1, :] = sel_idx[k]
        w_ref[k:k + 1, :] = sel_score[k] / total * ROUTED_SCALE
        rank_ref[k:k + 1, :] = jnp.sum(jnp.where(hits[k], base, 0.0), axis=0, keepdims=True).astype(I32)
    new_carry = carry_ref[...] + jnp.sum(multi_f, axis=1, keepdims=True)
    carry_ref[...] = new_carry
    cnt_ref[...] = new_carry.astype(I32)


def _router(x, wr_t, br, tt=512):
    n, d = x.shape
    tok = pl.BlockSpec((TOP_K, tt), lambda i: (0, i))
    return pl.pallas_call(
        functools.partial(_router_kernel, tt=tt),
        grid=(n // tt,),
        in_specs=[pl.BlockSpec((tt, d), lambda i: (i, 0)),
                  pl.BlockSpec(wr_t.shape, lambda i: (0, 0)),
                  pl.BlockSpec(br.shape, lambda i: (0, 0))],
        out_specs=[tok, tok, tok, pl.BlockSpec((N_EXPERTS, 1), lambda i: (0, 0))],
        out_shape=[jax.ShapeDtypeStruct((TOP_K, n), I32), jax.ShapeDtypeStruct((TOP_K, n), F32),
                   jax.ShapeDtypeStruct((TOP_K, n), I32), jax.ShapeDtypeStruct((N_EXPERTS, 1), I32)],
        scratch_shapes=[pltpu.VMEM((N_EXPERTS, 1), F32)],
        compiler_params=_params(("arbitrary",)),
        name="router",
    )(x, wr_t, br)


def _row_copy(src_ref, src_row, dst_ref, dst_row, sem):
    return pltpu.make_async_copy(src_ref.at[pl.ds(src_row, 1)], dst_ref.at[pl.ds(dst_row, 1)], sem)


def _dispatch_kernel(pstart_ref, pcnt_ref, nu_ref, idx_ref, rank_ref, x_ref, xs_ref, zero_ref, sem, *, tt, bm):
    i = pl.program_id(0)

    @pl.when(i == 0)
    def _():
        zero_ref[...] = jnp.zeros_like(zero_ref)

        def block_copy(row):
            return pltpu.make_async_copy(zero_ref, xs_ref.at[pl.ds(pl.multiple_of(row, bm), bm)], sem)

        def pad_copy(e):
            return block_copy(pstart_ref[e] + pcnt_ref[e] - bm)

        def start_pad(e, c):
            @pl.when(pcnt_ref[e] > 0)
            def _():
                pad_copy(e).start()
            return c

        def wait_pad(e, c):
            @pl.when(pcnt_ref[e] > 0)
            def _():
                pad_copy(e).wait()
            return c

        def start_tail(b, c):
            block_copy(b * bm).start()
            return c

        def wait_tail(b, c):
            block_copy(b * bm).wait()
            return c

        n_blocks = xs_ref.shape[0] // bm
        lax.fori_loop(0, N_EXPERTS, start_pad, 0)
        lax.fori_loop(nu_ref[0], n_blocks, start_tail, 0)
        lax.fori_loop(0, N_EXPERTS, wait_pad, 0)
        lax.fori_loop(nu_ref[0], n_blocks, wait_tail, 0)

    def start_rows(t, c):
        for k in range(TOP_K):
            dst = pstart_ref[idx_ref[k, t]] + rank_ref[k, t]
            _row_copy(x_ref, t, xs_ref, dst, sem).start()
        return c

    def wait_rows(t, c):
        for k in range(TOP_K):
            _row_copy(x_ref, 0, xs_ref, 0, sem).wait()
        return c

    lax.fori_loop(0, tt, start_rows, 0)
    lax.fori_loop(0, tt, wait_rows, 0)


def _dispatch(pstart, pcnt, n_used, idx, rank, x, p_rows, tt=256, bm=EXPERT_BM):
    n, d = x.shape
    smem_tok = pl.BlockSpec((TOP_K, tt), lambda i, ps, pc, nu: (0, i), memory_space=pltpu.SMEM)
    grid_spec = pltpu.PrefetchScalarGridSpec(
        num_scalar_prefetch=3,
        grid=(n // tt,),
        in_specs=[smem_tok, smem_tok, pl.BlockSpec((tt, d), lambda i, ps, pc, nu: (i, 0))],
        out_specs=pl.BlockSpec(memory_space=pl.ANY),
        scratch_shapes=[pltpu.VMEM((bm, d), x.dtype), pltpu.SemaphoreType.DMA(())],
    )
    return pl.pallas_call(
        functools.partial(_dispatch_kernel, tt=tt, bm=bm),
        grid_spec=grid_spec,
        out_shape=jax.ShapeDtypeStruct((p_rows, d), x.dtype),
        compiler_params=_params(("arbitrary",)),
        name="dispatch",
    )(pstart, pcnt, n_used, idx, rank, x)


def _expert_kernel(be_ref, nu_ref, x_ref, wg_ref, wu_ref, wd_ref, y_ref, wg_bf, wu_bf, wd_bf):
    b = pl.program_id(0)
    e = be_ref[b]
    prev = be_ref[jnp.maximum(b - 1, 0)]

    @pl.when(jnp.logical_or(b == 0, e != prev))
    def _():
        wg_bf[...] = wg_ref[0, 0].astype(BF16)
        wu_bf[...] = wu_ref[0, 0].astype(BF16)
        wd_bf[...] = wd_ref[0, 0].astype(BF16)

    @pl.when(b < nu_ref[0])
    def _():
        x = x_ref[...].astype(BF16)
        g = jnp.dot(x, wg_bf[...], preferred_element_type=F32)
        u = jnp.dot(x, wu_bf[...], preferred_element_type=F32)
        h = (g * jax.nn.sigmoid(g)) * u
        y_ref[...] = jnp.dot(h.astype(BF16), wd_bf[...], preferred_element_type=F32)

    @pl.when(b >= nu_ref[0])
    def _():
        y_ref[...] = jnp.zeros_like(y_ref)


def _experts(block_e, n_used, xs, w_gate, w_up, w_down, layer, bm=EXPERT_BM):
    p_rows, d = xs.shape
    de = w_gate.shape[3]
    wspec = lambda a: pl.BlockSpec((1, 1) + a.shape[2:], lambda b, be, nu: (layer, be[b], 0, 0))
    grid_spec = pltpu.PrefetchScalarGridSpec(
        num_scalar_prefetch=2,
        grid=(p_rows // bm,),
        in_specs=[pl.BlockSpec((bm, d), lambda b, be, nu: (jnp.minimum(b, nu[0] - 1), 0)),
                  wspec(w_gate), wspec(w_up), wspec(w_down)],
        out_specs=pl.BlockSpec((bm, d), lambda b, be, nu: (b, 0)),
        scratch_shapes=[pltpu.VMEM((d, de), BF16), pltpu.VMEM((d, de), BF16), pltpu.VMEM((de, d), BF16)],
    )
    return pl.pallas_call(
        _expert_kernel,
        grid_spec=grid_spec,
        out_shape=jax.ShapeDtypeStruct((p_rows, d), F32),
        compiler_params=_params(("arbitrary",)),
        name="expert",
    )(block_e, n_used, xs, w_gate, w_up, w_down)


def _combine_kernel(pstart_ref, idx_ref, rank_ref, x_ref, xb_ref, w_ref, ys_ref, sg_ref, su_ref, sd_ref,
                    g_ref, b_ref, o_ref, ob_ref, buf, sem, *, tt):
    def start_rows(t, c):
        for k in range(TOP_K):
            src = pstart_ref[idx_ref[k, t]] + rank_ref[k, t]
            _row_copy(ys_ref, src, buf.at[k], t, sem).start()
        return c

    def wait_rows(t, c):
        for k in range(TOP_K):
            _row_copy(ys_ref, 0, buf.at[k], 0, sem).wait()
        return c

    lax.fori_loop(0, tt, start_rows, 0)

    xb = xb_ref[...]
    g = jnp.dot(xb, sg_ref[...], preferred_element_type=F32)
    u = jnp.dot(xb, su_ref[...], preferred_element_type=F32)
    h = (g * jax.nn.sigmoid(g)) * u
    shared = jnp.dot(h.astype(BF16), sd_ref[...], preferred_element_type=F32)

    lax.fori_loop(0, tt, wait_rows, 0)

    w = w_ref[...]
    routed = w[:, 0:1] * buf[0]
    for k in range(1, TOP_K):
        routed += w[:, k:k + 1] * buf[k]
    y = _layer_norm(ALPHA * x_ref[...] + (routed + shared), g_ref[...], b_ref[...])
    o_ref[...] = y
    ob_ref[...] = y.astype(BF16)


def _combine(pstart, idx, rank, x, x_bf, w_tok, ys, sg, su, sd, g, b, tt=128):
    n, d = x.shape
    smem_tok = pl.BlockSpec((TOP_K, tt), lambda i, ps: (0, i), memory_space=pltpu.SMEM)
    rows = lambda a: pl.BlockSpec((tt, a.shape[1]), lambda i, ps: (i, 0))
    full = lambda a: pl.BlockSpec(a.shape, lambda i, ps: (0,) * a.ndim)
    grid_spec = pltpu.PrefetchScalarGridSpec(
        num_scalar_prefetch=1,
        grid=(n // tt,),
        in_specs=[smem_tok, smem_tok, rows(x), rows(x_bf), rows(w_tok), pl.BlockSpec(memory_space=pl.ANY),
                  full(sg), full(su), full(sd), full(g), full(b)],
        out_specs=[rows(x), rows(x)],
        scratch_shapes=[pltpu.VMEM((TOP_K, tt, d), F32), pltpu.SemaphoreType.DMA(())],
    )
    return pl.pallas_call(
        functools.partial(_combine_kernel, tt=tt),
        grid_spec=grid_spec,
        out_shape=[jax.ShapeDtypeStruct((n, d), F32), jax.ShapeDtypeStruct((n, d), BF16)],
        compiler_params=_params(("arbitrary",)),
        name="combine",
    )(pstart, idx, rank, x, x_bf, w_tok, ys, sg, su, sd, g, b)


def _block_plan(counts, n_pairs, bm=EXPERT_BM):
    n_blocks = -(-(n_pairs + N_EXPERTS * (bm - 1)) // bm)
    pcnt = (counts + bm - 1) // bm * bm
    pend = jnp.cumsum(pcnt)
    pstart = pend - pcnt
    n_used = pend[-1] // bm
    blk = jnp.arange(n_blocks, dtype=I32)
    first_row = jnp.minimum(blk, n_used - 1) * bm
    block_e = jnp.sum(pend[None, :] <= first_row[:, None], axis=1).astype(I32)
    block_e = jnp.minimum(block_e, N_EXPERTS - 1)
    return pstart.astype(I32), pcnt.astype(I32), block_e, n_used.astype(I32).reshape(1), n_blocks * bm


def kernel(x, w_in, conv_w, lam_q1, lam_k1, lam_q2, lam_k2, attn_sub_g, gmlp_v_g, gmlp_v_b, gmlp_w_s, gmlp_b_s,
           w_out, ln1_g, ln1_b, w_router, b_router, w_gate, w_up, w_down, ws_gate, ws_up, ws_down, ln2_g, ln2_b):
    batch, seq, d = x.shape
    n = batch * seq
    d_att = N_HEADS * HEAD_DIM
    xf = x.reshape(n, d)
    xb = xf.astype(BF16)
    for l in range(DEPTH):
        lam_init = 0.8 - 0.6 * math.exp(-0.3 * l)
        w_in_bf = w_in[l].astype(BF16)
        qkv = _proj(xb, w_in_bf[:, :3 * d_att], BF16, 1024, 512, "proj_qkv")
        rest = _proj(xb, w_in_bf[:, 3 * d_att:], F32, 1024, 512, "proj_rest")
        ya = _attention(qkv, lam_q1[l][None], lam_k1[l][None], lam_q2[l][None], lam_k2[l][None],
                        attn_sub_g[l][None], lam_init, batch, seq)
        yc, yg = _mix(rest, conv_w[l], gmlp_v_g[l][None], gmlp_v_b[l][None], gmlp_w_s[l],
                      gmlp_b_s[l][:, :, None], batch, seq)
        x1, x1b = _outproj(ya, yc, yg, xf, w_out[l].astype(BF16), ln1_g[l][None], ln1_b[l][None])

        idx, w_sel, rank, counts = _router(x1, w_router[l].T, b_router[l][:, None])
        pstart, pcnt, block_e, n_used, p_rows = _block_plan(counts[:, 0], n * TOP_K)
        xs = _dispatch(pstart, pcnt, n_used, idx, rank, x1, p_rows)
        ys = _experts(block_e, n_used, xs, w_gate, w_up, w_down, l)
        xf, xb = _combine(pstart, idx, rank, x1, x1b, w_sel.T, ys,
                          ws_gate[l].astype(BF16), ws_up[l].astype(BF16), ws_down[l].astype(BF16),
                          ln2_g[l][None], ln2_b[l][None])
    return xf.reshape(batch, seq, d)
```
